```python
import jax
import jax.numpy as jnp
from jax import lax
import numpy as np


D_MODEL = 2048
BATCH = 1
SEQ = 16384
DEPTH = 1

HEAD_DIM = 128
ATTN_WIDTH = 3 * D_MODEL // 4
N_ATTN_HEADS = ATTN_WIDTH // HEAD_DIM
ATTN_GROUPS = ((128, 1), (512, 4), (2048, 16))
N_GROUPS = len(ATTN_GROUPS)
HEADS_PER_GROUP = N_ATTN_HEADS // N_GROUPS
ATTN_OUT_WIDTH = HEADS_PER_GROUP * HEAD_DIM
BAND = 128
ROPE_THETA = 10000.0
POOL_WIDTH = D_MODEL // 4
POOL_WINDOWS = (2, 4, 8, 16)
POOL_GROUP = POOL_WIDTH // len(POOL_WINDOWS)
IN_WIDTH = 3 * ATTN_WIDTH + POOL_WIDTH + 2 * D_MODEL
PEER_HEADS = 8
PEER_QUERY_DIM = 256
PEER_HALF = PEER_QUERY_DIM // 2
PEER_N_KEYS = 128
PEER_N_EXPERTS = PEER_N_KEYS * PEER_N_KEYS
PEER_TOPK = 16
PEER_CHUNK = 128
EPS = 1e-6

kernel_name = 'hybrid_dilated_pool_peer_block'


def rms_norm(x, g):
    x32 = x.astype(jnp.float32)
    y = x32 * lax.rsqrt(jnp.mean(x32 * x32, axis=-1, keepdims=True) + EPS)
    return y.astype(x.dtype) * g


def apply_rope(t, pos):
    half = t.shape[-1] // 2
    inv = jnp.power(jnp.float32(ROPE_THETA), -jnp.arange(half, dtype=jnp.float32) / half)
    ang = pos[:, None] * inv[None, :]
    cos = jnp.cos(ang)[None, :, None, :]
    sin = jnp.sin(ang)[None, :, None, :]
    t32 = t.astype(jnp.float32)
    t1, t2 = t32[..., :half], t32[..., half:]
    return jnp.concatenate([t1 * cos - t2 * sin, t2 * cos + t1 * sin], axis=-1).astype(t.dtype)


def dilated_window_attention(q, k, v, dil, steps):
    B, S, H, Dh = q.shape
    L = S // dil
    nb = -(-L // BAND)
    Lp = nb * BAND

    def to_sub(t):
        t = t.reshape(B, L, dil, H, Dh).transpose(0, 2, 1, 3, 4)
        return jnp.pad(t, ((0, 0), (0, 0), (0, Lp - L), (0, 0), (0, 0)))

    def band(t):
        t = jnp.pad(t, ((0, 0), (0, 0), (BAND, 0), (0, 0), (0, 0))).reshape(B, dil, nb + 1, BAND, H, Dh)
        return jnp.concatenate([t[:, :, :-1], t[:, :, 1:]], axis=3)

    qb = to_sub(q).reshape(B, dil, nb, BAND, H, Dh)
    kb = band(to_sub(k))
    vb = band(to_sub(v))
    s = jnp.einsum('brnqhd,brnkhd->brnhqk', qb, kb).astype(jnp.float32) * (Dh ** -0.5)
    qi = jnp.arange(BAND)[:, None]
    kj = jnp.arange(2 * BAND)[None, :]
    dist = qi + BAND - kj
    key_idx = jnp.arange(nb)[:, None, None] * BAND + kj[None] - BAND
    valid = (dist[None] >= 0) & (dist[None] <= steps) & (key_idx >= 0)
    s = jnp.where(valid[None, None, :, None], s, -jnp.inf)
    m = jnp.max(s, axis=-1, keepdims=True)
    p = jnp.exp(s - m)
    denom = jnp.sum(p, axis=-1, keepdims=True)
    o = jnp.einsum('brnhqk,brnkhd->brnqhd', (p / denom).astype(v.dtype), vb)
    lse = (m + jnp.log(denom))[..., 0]
    o = o.reshape(B, dil, Lp, H, Dh)[:, :, :L].transpose(0, 2, 1, 3, 4).reshape(B, S, H, Dh)
    lse = lse.transpose(0, 1, 2, 4, 3).reshape(B, dil, Lp, H)[:, :, :L]
    lse = lse.transpose(0, 2, 1, 3).reshape(B, S, H)
    return o, lse


def dilated_mixture_attention(q, k, v):
    B, S, _, Dh = q.shape
    outs, lses = [], []
    for gi, (window, dil) in enumerate(ATTN_GROUPS):
        hs = slice(gi * HEADS_PER_GROUP, (gi + 1) * HEADS_PER_GROUP)
        o, l = dilated_window_attention(q[:, :, hs], k[:, :, hs], v[:, :, hs], dil, window // dil)
        outs.append(o)
        lses.append(l)
    o = jnp.stack(outs, axis=0).astype(jnp.float32)
    w = jax.nn.softmax(jnp.stack(lses, axis=0), axis=0)[..., None]
    y = jnp.sum(w * o, axis=0).astype(q.dtype)
    return y.reshape(B, S, ATTN_OUT_WIDTH)


def multiscale_pool(p, mix, scale):
    B, S, C = p.shape
    p32 = p.astype(jnp.float32)
    cs = jnp.concatenate([jnp.zeros((B, 1, C), jnp.float32), lax.cumsum(p32, axis=1)], axis=1)
    t = jnp.arange(S)
    outs = []
    for g, w in enumerate(POOL_WINDOWS):
        sl = slice(g * POOL_GROUP, (g + 1) * POOL_GROUP)
        c_g = cs[..., sl]
        lo = jnp.concatenate([jnp.zeros((B, w - 1, POOL_GROUP), jnp.float32), c_g[:, :S - w + 1]], axis=1)
        mean = (c_g[:, 1:] - lo) / jnp.minimum(t + 1, w).astype(jnp.float32)[None, :, None]
        outs.append(mean - p32[..., sl])
    d = jnp.stack(outs, axis=2).astype(p.dtype)
    y = jnp.einsum('bsgc,gcd->bsgd', d, mix).reshape(B, S, C)
    return y * scale


def peer_ffn(xn, w_query, sub_keys, w_down, w_up):
    B, S, D = xn.shape
    q = jnp.einsum('bsd,de->bse', xn, w_query).reshape(B, S, PEER_HEADS, 2, PEER_HALF)
    s = jnp.einsum('bshpd,hpnd->bshpn', q, sub_keys).astype(jnp.float32)
    top_s, top_i = lax.top_k(s, PEER_TOPK)
    cand = top_s[..., 0, :, None] + top_s[..., 1, None, :]
    cand_s, cand_i = lax.top_k(cand.reshape(B, S, PEER_HEADS, PEER_TOPK * PEER_TOPK), PEER_TOPK)
    i1 = jnp.take_along_axis(top_i[..., 0, :], cand_i // PEER_TOPK, axis=-1)
    i2 = jnp.take_along_axis(top_i[..., 1, :], cand_i % PEER_TOPK, axis=-1)
    expert = i1 * PEER_N_KEYS + i2
    gates = jax.nn.softmax(cand_s, axis=-1).astype(xn.dtype)
    n_chunk = (B * S) // PEER_CHUNK
    xs = xn.reshape(n_chunk, PEER_CHUNK, D)
    ids = expert.reshape(n_chunk, PEER_CHUNK, PEER_HEADS * PEER_TOPK)
    gs = gates.reshape(n_chunk, PEER_CHUNK, PEER_HEADS * PEER_TOPK)

    def one_block(args):
        xc, ic, gc = args
        u = jnp.take(w_down, ic, axis=0)
        a = jax.nn.gelu(jnp.einsum('cd,ced->ce', xc, u), approximate=False) * gc
        vv = jnp.take(w_up, ic, axis=0)
        return jnp.einsum('ce,ced->cd', a, vv)

    out = lax.map(one_block, (xs, ids, gs))
    return out.reshape(B, S, D)


def setup_inputs(seed: int = 0) -> dict:
    key = jax.random.key(seed)
    ks = jax.random.split(key, 17)
    f32 = jnp.float32

    def nrm(k, shape, scale):
        return jax.random.normal(k, shape, f32) * scale

    n_pool = len(POOL_WINDOWS)
    return {
        'x': nrm(ks[0], (BATCH, SEQ, D_MODEL), 1.0),
        'c': nrm(ks[1], (BATCH, D_MODEL), 1.0),
        'ada_w': nrm(ks[2], (DEPTH, D_MODEL, 6 * D_MODEL), 0.5 * D_MODEL ** -0.5),
        'ada_b': nrm(ks[3], (DEPTH, 6 * D_MODEL), 0.02),
        'norm1_g': 1.0 + nrm(ks[4], (DEPTH, D_MODEL), 0.02),
        'w_in': nrm(ks[5], (DEPTH, D_MODEL, IN_WIDTH), D_MODEL ** -0.5),
        'pool_mix': nrm(ks[6], (DEPTH, n_pool, POOL_GROUP, POOL_GROUP), POOL_GROUP ** -0.5),
        'pool_scale': 1.0 + nrm(ks[7], (DEPTH, POOL_WIDTH), 0.1),
        'w_attn_branch': nrm(ks[8], (DEPTH, ATTN_OUT_WIDTH, D_MODEL), ATTN_OUT_WIDTH ** -0.5),
        'w_pool_branch': nrm(ks[9], (DEPTH, POOL_WIDTH, D_MODEL), POOL_WIDTH ** -0.5),
        'w_out': nrm(ks[10], (DEPTH, D_MODEL, D_MODEL), D_MODEL ** -0.5),
        'norm2_g': 1.0 + nrm(ks[11], (DEPTH, D_MODEL), 0.02),
        'peer_wq': nrm(ks[12], (DEPTH, D_MODEL, PEER_HEADS * PEER_QUERY_DIM), D_MODEL ** -0.5),
        'peer_keys': nrm(ks[13], (DEPTH, PEER_HEADS, 2, PEER_N_KEYS, PEER_HALF), PEER_HALF ** -0.5),
        'peer_down': nrm(ks[14], (DEPTH, PEER_N_EXPERTS, D_MODEL), D_MODEL ** -0.5),
        'peer_up': nrm(ks[15], (DEPTH, PEER_N_EXPERTS, D_MODEL), PEER_HEADS ** -0.5),
        'final_g': 1.0 + nrm(ks[16], (D_MODEL,), 0.02),
    }


def reference(x, c, ada_w, ada_b, norm1_g, w_in, pool_mix, pool_scale, w_attn_branch, w_pool_branch, w_out, norm2_g, peer_wq, peer_keys, peer_down, peer_up, final_g):
    B, S, D = x.shape
    pos = jnp.arange(S, dtype=jnp.float32)
    h = x
    for layer in range(DEPTH):
        mod = jax.nn.silu(c) @ ada_w[layer] + ada_b[layer]
        shift1, scale1, gate1, shift2, scale2, gate2 = jnp.split(mod[:, None, :], 6, axis=-1)

        xn = rms_norm(h, norm1_g[layer]) * (1.0 + scale1) + shift1
        proj = xn @ w_in[layer]
        q = proj[..., :ATTN_WIDTH].reshape(B, S, N_ATTN_HEADS, HEAD_DIM)
        k = proj[..., ATTN_WIDTH:2 * ATTN_WIDTH].reshape(B, S, N_ATTN_HEADS, HEAD_DIM)
        v = proj[..., 2 * ATTN_WIDTH:3 * ATTN_WIDTH].reshape(B, S, N_ATTN_HEADS, HEAD_DIM)
        off = 3 * ATTN_WIDTH
        p_in = proj[..., off:off + POOL_WIDTH]
        gate_a = jax.nn.sigmoid(proj[..., off + POOL_WIDTH:off + POOL_WIDTH + D])
        gate_b = jax.nn.sigmoid(proj[..., off + POOL_WIDTH + D:])
        q = apply_rope(q, pos)
        k = apply_rope(k, pos)
        attn = dilated_mixture_attention(q, k, v)
        pool = multiscale_pool(p_in, pool_mix[layer], pool_scale[layer])
        merged = gate_a * (attn @ w_attn_branch[layer]) + gate_b * (pool @ w_pool_branch[layer])
        h = h + gate1 * (merged @ w_out[layer])

        xn2 = rms_norm(h, norm2_g[layer]) * (1.0 + scale2) + shift2
        h = h + gate2 * peer_ffn(xn2, peer_wq[layer], peer_keys[layer], peer_down[layer], peer_up[layer])
    return rms_norm(h, final_g)
```

```python
import functools

import jax
import jax.numpy as jnp
from jax import lax
from jax.experimental import pallas as pl
from jax.experimental.pallas import tpu as pltpu

F32 = jnp.float32
BF16 = jnp.bfloat16

HEAD_DIM = 128
N_HEADS = 12
ATTN_WIDTH = N_HEADS * HEAD_DIM
GROUP_DILATIONS = (1, 4, 16)
GROUP_WIDTH = 4 * HEAD_DIM
BAND = 128
ROPE_THETA = 10000.0
POOL_WINDOWS = (2, 4, 8, 16)
POOL_HALO = 16
PEER_HEADS = 8
PEER_KEYS = 128
PEER_TOPK = 16
PEER_SLOTS = PEER_HEADS * PEER_TOPK
EPS = 1e-6
NEG = -1e30

LANES = 128
SUBLANES = 8
VMEM_LIMIT = 56 * 1024 * 1024


def _cparams(sem):
    return pltpu.CompilerParams(dimension_semantics=sem, vmem_limit_bytes=VMEM_LIMIT)


def _ada_kernel(c_ref, w_ref, b_ref, o_ref):
    c = c_ref[...]
    s = c * jax.nn.sigmoid(c)
    o_ref[...] = jnp.sum(s * w_ref[...], axis=0, keepdims=True) + b_ref[...]


def _ada(c_col, w, b):
    d, n = w.shape
    tn = 1024
    return pl.pallas_call(
        _ada_kernel,
        grid=(n // tn,),
        in_specs=[pl.BlockSpec((d, 1), lambda j: (0, 0)),
                  pl.BlockSpec((d, tn), lambda j: (0, j)),
                  pl.BlockSpec((1, tn), lambda j: (0, j))],
        out_specs=pl.BlockSpec((1, tn), lambda j: (0, j)),
        out_shape=jax.ShapeDtypeStruct((1, n), F32),
        compiler_params=_cparams(("arbitrary",)),
        name="ada_mod",
    )(c_col, w, b)


def _modulated_norm(x, g, scale, shift):
    y = x * lax.rsqrt(jnp.mean(x * x, axis=-1, keepdims=True) + EPS)
    return y * g * (1.0 + scale) + shift


def _proj_kernel(x_ref, g_ref, sc_ref, sh_ref, w_ref, cos_ref, sin_ref,
                 qkv_ref, pool_ref, gates_ref, xn_ref, *, n_qk, n_qkv):
    j = pl.program_id(1)

    @pl.when(j == 0)
    def _():
        xn_ref[...] = _modulated_norm(x_ref[...], g_ref[...], sc_ref[...], sh_ref[...]).astype(BF16)

    acc = jnp.dot(xn_ref[...], w_ref[...], preferred_element_type=F32)

    @pl.when(j < n_qk)
    def _():
        cos = cos_ref[...]
        sin = sin_ref[...]
        for hh in range(GROUP_WIDTH // HEAD_DIM):
            t = acc[:, hh * HEAD_DIM:(hh + 1) * HEAD_DIM]
            r = pltpu.roll(t, HEAD_DIM // 2, axis=1)
            qkv_ref[:, hh * HEAD_DIM:(hh + 1) * HEAD_DIM] = (t * cos + r * sin).astype(BF16)

    @pl.when(jnp.logical_and(j >= n_qk, j < n_qkv))
    def _():
        qkv_ref[...] = acc.astype(BF16)

    @pl.when(j == n_qkv)
    def _():
        pool_ref[...] = acc

    @pl.when(j > n_qkv)
    def _():
        gates_ref[...] = jax.nn.sigmoid(acc)


def _proj(x, g, scale, shift, w_bf16, cos, sin):
    s, d = x.shape
    n = w_bf16.shape[1]
    tn = GROUP_WIDTH
    tm = min(1024, s)
    n_qk = 2 * ATTN_WIDTH // tn
    n_qkv = 3 * ATTN_WIDTH // tn
    n_gate = 2 * d // tn
    nj = n // tn
    assert nj == n_qkv + 1 + n_gate
    vec = pl.BlockSpec((1, d), lambda i, j: (0, 0))
    return pl.pallas_call(
        functools.partial(_proj_kernel, n_qk=n_qk, n_qkv=n_qkv),
        grid=(s // tm, nj),
        in_specs=[pl.BlockSpec((tm, d), lambda i, j: (i, 0)), vec, vec, vec,
                  pl.BlockSpec((d, tn), lambda i, j: (0, j)),
                  pl.BlockSpec((tm, HEAD_DIM), lambda i, j: (i, 0)),
                  pl.BlockSpec((tm, HEAD_DIM), lambda i, j: (i, 0))],
        out_specs=[pl.BlockSpec((tm, tn), lambda i, j: (i, jnp.minimum(j, n_qkv - 1))),
                   pl.BlockSpec((tm, tn), lambda i, j: (i, 0)),
                   pl.BlockSpec((tm, tn), lambda i, j: (i, jnp.clip(j - n_qkv - 1, 0, n_gate - 1)))],
        out_shape=[jax.ShapeDtypeStruct((s, 3 * ATTN_WIDTH), BF16),
                   jax.ShapeDtypeStruct((s, tn), F32),
                   jax.ShapeDtypeStruct((s, 2 * d), F32)],
        scratch_shapes=[pltpu.VMEM((tm, d), BF16)],
        compiler_params=_cparams(("arbitrary", "arbitrary")),
        name="norm1_proj",
    )(x, g, scale, shift, w_bf16, cos, sin)


def _attn_kernel(q_ref, kc_ref, kp_ref, vc_ref, vp_ref, o_ref, l_ref, *, tq):
    n = pl.program_id(1)
    row = lax.broadcasted_iota(jnp.int32, (BAND, 2 * BAND), 0)
    col = lax.broadcasted_iota(jnp.int32, (BAND, 2 * BAND), 1)
    band = jnp.logical_and(col >= row, col <= row + BAND)
    first = jnp.logical_and(band, jnp.logical_or(col >= BAND, n > 0))
    scale = HEAD_DIM ** -0.5
    for i in range(tq // BAND):
        valid = first if i == 0 else band
        for hh in range(GROUP_WIDTH // HEAD_DIM):
            cs = slice(hh * HEAD_DIM, (hh + 1) * HEAD_DIM)
            q = q_ref[i * BAND:(i + 1) * BAND, cs]
            if i == 0:
                k = jnp.concatenate([kp_ref[:, cs], kc_ref[0:BAND, cs]], axis=0)
                v = jnp.concatenate([vp_ref[:, cs], vc_ref[0:BAND, cs]], axis=0)
            else:
                k = kc_ref[(i - 1) * BAND:(i + 1) * BAND, cs]
                v = vc_ref[(i - 1) * BAND:(i + 1) * BAND, cs]
            s = lax.dot_general(q, k, (((1,), (1,)), ((), ())), preferred_element_type=F32) * scale
            s = jnp.where(valid, s, NEG)
            m = jnp.max(s, axis=-1, keepdims=True)
            p = jnp.exp(s - m)
            den = jnp.sum(p, axis=-1, keepdims=True)
            o = jnp.dot(p.astype(BF16), v, preferred_element_type=F32)
            o_ref[i * BAND:(i + 1) * BAND, cs] = o / den
            l_ref[i * BAND:(i + 1) * BAND, cs] = jnp.broadcast_to(m + jnp.log(den), (BAND, HEAD_DIM))


def _attn_group(qkv, gi, dil):
    s = qkv.shape[0]
    length = s // dil
    tq = min(512, length)
    cols_per_pos = 3 * ATTN_WIDTH // GROUP_WIDTH
    kv_blk = ATTN_WIDTH // GROUP_WIDTH
    qkv_d = qkv.reshape(length, dil * 3 * ATTN_WIDTH)
    sub = tq // BAND

    def cur(off):
        return pl.BlockSpec((tq, GROUP_WIDTH), lambda r, n: (n, r * cols_per_pos + off))

    def prev(off):
        return pl.BlockSpec((BAND, GROUP_WIDTH),
                            lambda r, n: (jnp.maximum(n * sub - 1, 0), r * cols_per_pos + off))

    out_spec = pl.BlockSpec((tq, GROUP_WIDTH), lambda r, n: (n, r))
    o, lse = pl.pallas_call(
        functools.partial(_attn_kernel, tq=tq),
        grid=(dil, length // tq),
        in_specs=[cur(gi), cur(kv_blk + gi), prev(kv_blk + gi), cur(2 * kv_blk + gi), prev(2 * kv_blk + gi)],
        out_specs=[out_spec, out_spec],
        out_shape=[jax.ShapeDtypeStruct((length, dil * GROUP_WIDTH), F32)] * 2,
        compiler_params=_cparams(("arbitrary", "arbitrary")),
        name=f"dilated_attn_d{dil}",
    )(qkv_d, qkv_d, qkv_d, qkv_d, qkv_d)
    return o.reshape(s, GROUP_WIDTH), lse.reshape(s, GROUP_WIDTH)


def _merge_kernel(o0, o1, o2, l0, l1, l2, p_ref, ph_ref, ga_ref, gb_ref, x_ref,
                  wab_ref, wpb_ref, wout_ref, mix_ref, psc_ref, g1_ref, n2_ref, sc2_ref, sh2_ref,
                  h_ref, xn2_ref, *, tm):
    i = pl.program_id(0)
    la, lb, lc = l0[...], l1[...], l2[...]
    lm = jnp.maximum(jnp.maximum(la, lb), lc)
    wa, wb, wc = jnp.exp(la - lm), jnp.exp(lb - lm), jnp.exp(lc - lm)
    attn = (wa * o0[...] + wb * o1[...] + wc * o2[...]) / (wa + wb + wc)

    halo = jnp.where(i > 0, ph_ref[...], 0.0)
    p_ext = jnp.concatenate([halo, p_ref[...]], axis=0)
    t_pos = i * tm + lax.broadcasted_iota(jnp.int32, (tm, 1), 0)
    sums = p_ext
    win = 1
    pooled = []
    for g, w in enumerate(POOL_WINDOWS):
        while win < w:
            sums = sums + pltpu.roll(sums, win, axis=0)
            win *= 2
        cs = slice(g * LANES, (g + 1) * LANES)
        cnt = jnp.minimum(t_pos + 1, w).astype(F32)
        d = sums[POOL_HALO:, cs] / cnt - p_ext[POOL_HALO:, cs]
        y = jnp.dot(d.astype(BF16), mix_ref[g], preferred_element_type=F32)
        pooled.append(y)
    pool = jnp.concatenate(pooled, axis=1) * psc_ref[...]

    a_br = jnp.dot(attn.astype(BF16), wab_ref[...], preferred_element_type=F32)
    p_br = jnp.dot(pool.astype(BF16), wpb_ref[...], preferred_element_type=F32)
    merged = ga_ref[...] * a_br + gb_ref[...] * p_br
    h = x_ref[...] + g1_ref[...] * jnp.dot(merged.astype(BF16), wout_ref[...], preferred_element_type=F32)
    h_ref[...] = h
    xn2_ref[...] = _modulated_norm(h, n2_ref[...], sc2_ref[...], sh2_ref[...])


def _merge(os_, ls_, pool_in, gates, x, wab, wpb, wout, mix, pscale, gate1, n2g, scale2, shift2):
    s, d = x.shape
    tm = min(256, s)
    gw = GROUP_WIDTH
    blk = pl.BlockSpec((tm, gw), lambda i: (i, 0))
    vec = pl.BlockSpec((1, d), lambda i: (0, 0))

    def full(shape):
        return pl.BlockSpec(shape, lambda i: (0,) * len(shape))

    return pl.pallas_call(
        functools.partial(_merge_kernel, tm=tm),
        grid=(s // tm,),
        in_specs=[blk] * 6 + [
            blk,
            pl.BlockSpec((POOL_HALO, gw), lambda i: (jnp.maximum(i * (tm // POOL_HALO) - 1, 0), 0)),
            pl.BlockSpec((tm, d), lambda i: (i, 0)),
            pl.BlockSpec((tm, d), lambda i: (i, 1)),
            pl.BlockSpec((tm, d), lambda i: (i, 0)),
            full((gw, d)), full((gw, d)), full((d, d)), full(mix.shape), full((1, gw)),
            vec, vec, vec, vec],
        out_specs=[pl.BlockSpec((tm, d), lambda i: (i, 0))] * 2,
        out_shape=[jax.ShapeDtypeStruct((s, d), F32)] * 2,
        compiler_params=_cparams(("arbitrary",)),
        name="merge_outproj_norm2",
    )(*os_, *ls_, pool_in, pool_in, gates, gates, x, wab, wpb, wout, mix, pscale, gate1, n2g, scale2, shift2)


def _top16(s, n_rows):
    iota = lax.broadcasted_iota(jnp.int32, s.shape, 0)
    vals, poss = [], []
    for _ in range(PEER_TOPK):
        m = jnp.max(s, axis=0, keepdims=True)
        pos = jnp.min(jnp.where(s == m, iota, n_rows), axis=0, keepdims=True)
        vals.append(m)
        poss.append(pos)
        s = jnp.where(iota == pos, NEG, s)
    return vals, poss


def _route_kernel(x_ref, wq_ref, keys_ref, ids_ref, g_ref, xb_ref):
    hd = pl.program_id(1)

    @pl.when(hd == 0)
    def _():
        xb_ref[...] = x_ref[...].astype(BF16)

    qp = jnp.dot(xb_ref[...], wq_ref[...], preferred_element_type=F32)
    tops = []
    for half in range(2):
        qh = qp[:, half * LANES:(half + 1) * LANES].astype(BF16)
        st = lax.dot_general(keys_ref[0, half], qh, (((1,), (1,)), ((), ())),
                             preferred_element_type=F32)
        tops.append(_top16(st, PEER_KEYS))
    (v1, i1), (v2, i2) = tops
    v2c = jnp.concatenate(v2, axis=0)
    i2c = jnp.concatenate(i2, axis=0)
    cand = jnp.concatenate([v1[a] + v2c for a in range(PEER_TOPK)], axis=0)
    eid = jnp.concatenate([i1[a] * PEER_KEYS + i2c for a in range(PEER_TOPK)], axis=0)
    n_c = PEER_TOPK * PEER_TOPK
    iota = lax.broadcasted_iota(jnp.int32, cand.shape, 0)
    sel_v, sel_e = [], []
    for _ in range(PEER_TOPK):
        m = jnp.max(cand, axis=0, keepdims=True)
        pos = jnp.min(jnp.where(cand == m, iota, n_c), axis=0, keepdims=True)
        hit = iota == pos
        sel_v.append(m)
        sel_e.append(jnp.sum(jnp.where(hit, eid, 0), axis=0, keepdims=True))
        cand = jnp.where(hit, NEG, cand)
    sv = jnp.concatenate(sel_v, axis=0)
    ex = jnp.exp(sv - sv[0:1, :])
    g_ref[...] = ex / jnp.sum(ex, axis=0, keepdims=True)
    ids_ref[...] = jnp.concatenate(sel_e, axis=0)


def _route(xn2, wq_bf16, keys_bf16):
    s, d = xn2.shape
    tm = min(512, s)
    qd = 2 * LANES
    out_spec = pl.BlockSpec((PEER_TOPK, tm), lambda i, h: (h, i))
    return pl.pallas_call(
        _route_kernel,
        grid=(s // tm, PEER_HEADS),
        in_specs=[pl.BlockSpec((tm, d), lambda i, h: (i, 0)),
                  pl.BlockSpec((d, qd), lambda i, h: (0, h)),
                  pl.BlockSpec((1, 2, PEER_KEYS, LANES), lambda i, h: (h, 0, 0, 0))],
        out_specs=[out_spec, out_spec],
        out_shape=[jax.ShapeDtypeStruct((PEER_SLOTS, s), jnp.int32),
                   jax.ShapeDtypeStruct((PEER_SLOTS, s), F32)],
        scratch_shapes=[pltpu.VMEM((tm, d), BF16)],
        compiler_params=_cparams(("arbitrary", "arbitrary")),
        name="peer_route",
    )(xn2, wq_bf16, keys_bf16)


PEER_TOK_BLOCK = 128
PEER_SUB = 4
PEER_ROWS = PEER_SUB * PEER_SLOTS
ROW_TILES = 2 * 2048 // LANES
HALF_TILES = ROW_TILES // 2
N_ACC = 4


def _erf_gelu(x):
    return 0.5 * x * (1.0 + lax.erf(x * (2.0 ** -0.5)))


def _peer_kernel(ids_hbm, tab_hbm, x_ref, h_ref, gates_ref, g2_ref, fg_ref, o_ref,
                 ids_smem, buf, pscr, acts, ascr, sem_ids, sem, *, d_model):
    step = pl.program_id(0)
    n_ids = PEER_TOK_BLOCK * PEER_SLOTS
    ids_copy = pltpu.make_async_copy(ids_hbm.at[pl.ds(step * n_ids, n_ids)], ids_smem, sem_ids)
    ids_copy.start()
    ids_copy.wait()

    def row_copy(e, slot, r):
        return pltpu.make_async_copy(tab_hbm.at[e], buf.at[slot, r], sem.at[slot])

    def issue(j, slot):
        def body(r8, carry):
            for u in range(SUBLANES):
                r = r8 * SUBLANES + u
                row_copy(ids_smem[j * PEER_ROWS + r], slot, r).start()
            return carry
        lax.fori_loop(0, PEER_ROWS // SUBLANES, body, 0)

    def wait_all(slot):
        def body(r8, carry):
            for u in range(SUBLANES):
                row_copy(0, slot, r8 * SUBLANES + u).wait()
            return carry
        lax.fori_loop(0, PEER_ROWS // SUBLANES, body, 0)

    lane = lax.broadcasted_iota(jnp.int32, (SUBLANES, LANES), 1)
    n_grp = PEER_SLOTS // SUBLANES

    def grp_rows(g):
        return pl.ds(pl.multiple_of(g * SUBLANES, SUBLANES), SUBLANES)

    def compute(j, slot):
        acts[...] = jnp.zeros_like(acts)

        def down_token(tt, carry):
            tl = j * PEER_SUB + tt
            x = x_ref[tl]
            x0, x1 = x[0:SUBLANES], x[SUBLANES:]

            def down_group(g, carry):
                for u in range(SUBLANES):
                    dn = buf[slot, tt * PEER_SLOTS + g * SUBLANES + u, 0:HALF_TILES, :]
                    pscr[u * SUBLANES:(u + 1) * SUBLANES, :] = dn[0:SUBLANES] * x0 + dn[SUBLANES:] * x1
                q = pscr[pl.ds(0, SUBLANES, stride=SUBLANES), :]
                for sl in range(1, SUBLANES):
                    q = q + pscr[pl.ds(sl, SUBLANES, stride=SUBLANES), :]
                a = jnp.sum(q, axis=1, keepdims=True)
                acts[grp_rows(g), :] = jnp.where(lane == tl, a, acts[grp_rows(g), :])
                return carry

            return lax.fori_loop(0, n_grp, down_group, carry)

        lax.fori_loop(0, PEER_SUB, down_token, 0)
        acts[...] = _erf_gelu(acts[...]) * gates_ref[...]

        def up_token(tt, carry):
            tl = j * PEER_SUB + tt
            for g in range(n_grp):
                rows = slice(g * SUBLANES, (g + 1) * SUBLANES)
                colv = jnp.sum(jnp.where(lane == tl, acts[rows, :], 0.0), axis=1, keepdims=True)
                ascr[rows, :] = jnp.broadcast_to(colv, (SUBLANES, LANES))

            def up_group(g, acc):
                acc = list(acc)
                for u in range(SUBLANES):
                    w = jnp.broadcast_to(ascr[pl.ds(g * SUBLANES + u, 1), :], (HALF_TILES, LANES))
                    up = buf[slot, tt * PEER_SLOTS + g * SUBLANES + u, HALF_TILES:, :]
                    acc[u % N_ACC] = acc[u % N_ACC] + w * up
                return tuple(acc)

            acc = lax.fori_loop(0, n_grp, up_group, (jnp.zeros((HALF_TILES, LANES), F32),) * N_ACC)
            peer = (acc[0] + acc[1]) + (acc[2] + acc[3])
            y = h_ref[tl] + g2_ref[...] * peer
            ms = jnp.sum(jnp.sum(y * y, axis=1, keepdims=True), axis=0, keepdims=True) / d_model
            o_ref[tl] = y * lax.rsqrt(ms + EPS) * fg_ref[...]
            return carry

        lax.fori_loop(0, PEER_SUB, up_token, 0)

    n_sub = PEER_TOK_BLOCK // PEER_SUB
    issue(0, 0)

    def sub_block(j, carry):
        slot = j % 2

        @pl.when(j + 1 < n_sub)
        def _():
            issue(j + 1, 1 - slot)

        wait_all(slot)
        compute(j, slot)
        return carry

    lax.fori_loop(0, n_sub, sub_block, 0)


def _peer(ids_flat, table, xn2_t, h_t, gates_et, gate2_t, fg_t):
    s = xn2_t.shape[0]
    d_model = xn2_t.shape[1] * xn2_t.shape[2]
    tb = PEER_TOK_BLOCK
    tok = pl.BlockSpec((tb, HALF_TILES, LANES), lambda i: (i, 0, 0))
    vec = pl.BlockSpec((HALF_TILES, LANES), lambda i: (0, 0))
    return pl.pallas_call(
        functools.partial(_peer_kernel, d_model=d_model),
        grid=(s // tb,),
        in_specs=[pl.BlockSpec(memory_space=pl.ANY), pl.BlockSpec(memory_space=pl.ANY),
                  tok, tok, pl.BlockSpec((PEER_SLOTS, tb), lambda i: (0, i)), vec, vec],
        out_specs=tok,
        out_shape=jax.ShapeDtypeStruct((s, HALF_TILES, LANES), F32),
        scratch_shapes=[pltpu.SMEM((tb * PEER_SLOTS,), jnp.int32),
                        pltpu.VMEM((2, PEER_ROWS, ROW_TILES, LANES), F32),
                        pltpu.VMEM((SUBLANES * SUBLANES, LANES), F32),
                        pltpu.VMEM((PEER_SLOTS, LANES), F32),
                        pltpu.VMEM((PEER_SLOTS, LANES), F32),
                        pltpu.SemaphoreType.DMA(()),
                        pltpu.SemaphoreType.DMA((2,))],
        compiler_params=_cparams(("arbitrary",)),
        name="peer_experts_final_norm",
    )(ids_flat, table, xn2_t, h_t, gates_et, gate2_t, fg_t)


def kernel(x, c, ada_w, ada_b, norm1_g, w_in, pool_mix, pool_scale, w_attn_branch, w_pool_branch,
           w_out, norm2_g, peer_wq, peer_keys, peer_down, peer_up, final_g):
    b, s, d = x.shape
    assert b == 1 and ada_w.shape[0] == 1
    x2 = x[0]

    mod = _ada(c.reshape(d, 1), ada_w[0], ada_b[0].reshape(1, -1))
    shift1, scale1, gate1, shift2, scale2, gate2 = [mod[:, i * d:(i + 1) * d] for i in range(6)]

    half = HEAD_DIM // 2
    pos = jnp.arange(s, dtype=F32)
    inv = jnp.power(jnp.float32(ROPE_THETA), -jnp.arange(half, dtype=F32) / half)
    ang = pos[:, None] * inv[None, :]
    cos = jnp.concatenate([jnp.cos(ang), jnp.cos(ang)], axis=1)
    sin = jnp.concatenate([-jnp.sin(ang), jnp.sin(ang)], axis=1)

    qkv, pool_in, gates = _proj(x2, norm1_g, scale1, shift1, w_in[0].astype(BF16), cos, sin)

    outs, lses = [], []
    for gi, dil in enumerate(GROUP_DILATIONS):
        o, l = _attn_group(qkv, gi, dil)
        outs.append(o)
        lses.append(l)

    h, xn2 = _merge(outs, lses, pool_in, gates, x2,
                    w_attn_branch[0].astype(BF16), w_pool_branch[0].astype(BF16), w_out[0].astype(BF16),
                    pool_mix[0].astype(BF16), pool_scale, gate1, norm2_g, scale2, shift2)

    ids_et, gates_et = _route(xn2, peer_wq[0].astype(BF16), peer_keys[0].astype(BF16))

    table = jnp.concatenate([peer_down[0], peer_up[0]], axis=1).reshape(-1, ROW_TILES, LANES)
    ids_flat = ids_et.T.reshape(-1)
    as_tiles = lambda a: a.reshape(-1, HALF_TILES, LANES)
    out = _peer(ids_flat, table, as_tiles(xn2), as_tiles(h), gates_et,
                gate2.reshape(HALF_TILES, LANES), final_g.reshape(HALF_TILES, LANES))
    return out.reshape(b, s, d)
```

```python
import functools

import jax
import jax.numpy as jnp
from jax import lax
from jax.experimental import pallas as pl
from jax.experimental.pallas import tpu as pltpu

F32 = jnp.float32
BF16 = jnp.bfloat16

HEAD_DIM = 128
N_HEADS = 12
ATTN_WIDTH = N_HEADS * HEAD_DIM
GROUP_DILATIONS = (1, 4, 16)
GROUP_WIDTH = 4 * HEAD_DIM
BAND = 128
ROPE_THETA = 10000.0
POOL_WINDOWS = (2, 4, 8, 16)
POOL_HALO = 16
PEER_HEADS = 8
PEER_KEYS = 128
PEER_TOPK = 16
PEER_SLOTS = PEER_HEADS * PEER_TOPK
EPS = 1e-6
NEG = -1e30

LANES = 128
SUBLANES = 8
VMEM_LIMIT = 56 * 1024 * 1024


def _cparams(sem):
    return pltpu.CompilerParams(dimension_semantics=sem, vmem_limit_bytes=VMEM_LIMIT)


def _ada_kernel(c_ref, w_ref, b_ref, o_ref):
    c = c_ref[...]
    s = c * jax.nn.sigmoid(c)
    o_ref[...] = jnp.sum(s * w_ref[...], axis=0, keepdims=True) + b_ref[...]


def _ada(c_col, w, b):
    d, n = w.shape
    tn = 1024
    return pl.pallas_call(
        _ada_kernel,
        grid=(n // tn,),
        in_specs=[pl.BlockSpec((d, 1), lambda j: (0, 0)),
                  pl.BlockSpec((d, tn), lambda j: (0, j)),
                  pl.BlockSpec((1, tn), lambda j: (0, j))],
        out_specs=pl.BlockSpec((1, tn), lambda j: (0, j)),
        out_shape=jax.ShapeDtypeStruct((1, n), F32),
        compiler_params=_cparams(("arbitrary",)),
        name="ada_mod",
    )(c_col, w, b)


def _modulated_norm(x, g, scale, shift):
    y = x * lax.rsqrt(jnp.mean(x * x, axis=-1, keepdims=True) + EPS)
    return y * g * (1.0 + scale) + shift


def _proj_kernel(x_ref, g_ref, sc_ref, sh_ref, w_ref, cos_ref, sin_ref,
                 qkv_ref, pool_ref, gates_ref, xn_ref, *, n_qk, n_qkv):
    j = pl.program_id(1)

    @pl.when(j == 0)
    def _():
        xn_ref[...] = _modulated_norm(x_ref[...], g_ref[...], sc_ref[...], sh_ref[...]).astype(BF16)

    acc = jnp.dot(xn_ref[...], w_ref[...], preferred_element_type=F32)

    @pl.when(j < n_qk)
    def _():
        cos = cos_ref[...]
        sin = sin_ref[...]
        for hh in range(GROUP_WIDTH // HEAD_DIM):
            t = acc[:, hh * HEAD_DIM:(hh + 1) * HEAD_DIM]
            r = pltpu.roll(t, HEAD_DIM // 2, axis=1)
            qkv_ref[:, hh * HEAD_DIM:(hh + 1) * HEAD_DIM] = (t * cos + r * sin).astype(BF16)

    @pl.when(jnp.logical_and(j >= n_qk, j < n_qkv))
    def _():
        qkv_ref[...] = acc.astype(BF16)

    @pl.when(j == n_qkv)
    def _():
        pool_ref[...] = acc

    @pl.when(j > n_qkv)
    def _():
        gates_ref[...] = jax.nn.sigmoid(acc)


def _proj(x, g, scale, shift, w_bf16, cos, sin):
    s, d = x.shape
    n = w_bf16.shape[1]
    tn = GROUP_WIDTH
    tm = min(1024, s)
    n_qk = 2 * ATTN_WIDTH // tn
    n_qkv = 3 * ATTN_WIDTH // tn
    n_gate = 2 * d // tn
    nj = n // tn
    assert nj == n_qkv + 1 + n_gate
    vec = pl.BlockSpec((1, d), lambda i, j: (0, 0))
    return pl.pallas_call(
        functools.partial(_proj_kernel, n_qk=n_qk, n_qkv=n_qkv),
        grid=(s // tm, nj),
        in_specs=[pl.BlockSpec((tm, d), lambda i, j: (i, 0)), vec, vec, vec,
                  pl.BlockSpec((d, tn), lambda i, j: (0, j)),
                  pl.BlockSpec((tm, HEAD_DIM), lambda i, j: (i, 0)),
                  pl.BlockSpec((tm, HEAD_DIM), lambda i, j: (i, 0))],
        out_specs=[pl.BlockSpec((tm, tn), lambda i, j: (i, jnp.minimum(j, n_qkv - 1))),
                   pl.BlockSpec((tm, tn), lambda i, j: (i, 0)),
                   pl.BlockSpec((tm, tn), lambda i, j: (i, jnp.clip(j - n_qkv - 1, 0, n_gate - 1)))],
        out_shape=[jax.ShapeDtypeStruct((s, 3 * ATTN_WIDTH), BF16),
                   jax.ShapeDtypeStruct((s, tn), F32),
                   jax.ShapeDtypeStruct((s, 2 * d), F32)],
        scratch_shapes=[pltpu.VMEM((tm, d), BF16)],
        compiler_params=_cparams(("arbitrary", "arbitrary")),
        name="norm1_proj",
    )(x, g, scale, shift, w_bf16, cos, sin)


def _attn_kernel(q_ref, kc_ref, kp_ref, vc_ref, vp_ref, o_ref, l_ref, *, tq):
    n = pl.program_id(1)
    row = lax.broadcasted_iota(jnp.int32, (BAND, 2 * BAND), 0)
    col = lax.broadcasted_iota(jnp.int32, (BAND, 2 * BAND), 1)
    band = jnp.logical_and(col >= row, col <= row + BAND)
    first = jnp.logical_and(band, jnp.logical_or(col >= BAND, n > 0))
    scale = HEAD_DIM ** -0.5
    for i in range(tq // BAND):
        valid = first if i == 0 else band
        for hh in range(GROUP_WIDTH // HEAD_DIM):
            cs = slice(hh * HEAD_DIM, (hh + 1) * HEAD_DIM)
            q = q_ref[i * BAND:(i + 1) * BAND, cs]
            if i == 0:
                k = jnp.concatenate([kp_ref[:, cs], kc_ref[0:BAND, cs]], axis=0)
                v = jnp.concatenate([vp_ref[:, cs], vc_ref[0:BAND, cs]], axis=0)
            else:
                k = kc_ref[(i - 1) * BAND:(i + 1) * BAND, cs]
                v = vc_ref[(i - 1) * BAND:(i + 1) * BAND, cs]
            s = lax.dot_general(q, k, (((1,), (1,)), ((), ())), preferred_element_type=F32) * scale
            s = jnp.where(valid, s, NEG)
            m = jnp.max(s, axis=-1, keepdims=True)
            p = jnp.exp(s - m)
            den = jnp.sum(p, axis=-1, keepdims=True)
            o = jnp.dot(p.astype(BF16), v, preferred_element_type=F32)
            o_ref[i * BAND:(i + 1) * BAND, cs] = o / den
            l_ref[i * BAND:(i + 1) * BAND, cs] = jnp.broadcast_to(m + jnp.log(den), (BAND, HEAD_DIM))


def _attn_group(qkv, gi, dil):
    s = qkv.shape[0]
    length = s // dil
    tq = min(512, length)
    cols_per_pos = 3 * ATTN_WIDTH // GROUP_WIDTH
    kv_blk = ATTN_WIDTH // GROUP_WIDTH
    qkv_d = qkv.reshape(length, dil * 3 * ATTN_WIDTH)
    sub = tq // BAND

    def cur(off):
        return pl.BlockSpec((tq, GROUP_WIDTH), lambda r, n: (n, r * cols_per_pos + off))

    def prev(off):
        return pl.BlockSpec((BAND, GROUP_WIDTH),
                            lambda r, n: (jnp.maximum(n * sub - 1, 0), r * cols_per_pos + off))

    out_spec = pl.BlockSpec((tq, GROUP_WIDTH), lambda r, n: (n, r))
    o, lse = pl.pallas_call(
        functools.partial(_attn_kernel, tq=tq),
        grid=(dil, length // tq),
        in_specs=[cur(gi), cur(kv_blk + gi), prev(kv_blk + gi), cur(2 * kv_blk + gi), prev(2 * kv_blk + gi)],
        out_specs=[out_spec, out_spec],
        out_shape=[jax.ShapeDtypeStruct((length, dil * GROUP_WIDTH), F32)] * 2,
        compiler_params=_cparams(("arbitrary", "arbitrary")),
        name=f"dilated_attn_d{dil}",
    )(qkv_d, qkv_d, qkv_d, qkv_d, qkv_d)
    return o.reshape(s, GROUP_WIDTH), lse.reshape(s, GROUP_WIDTH)


def _merge_kernel(o0, o1, o2, l0, l1, l2, p_ref, ph_ref, ga_ref, gb_ref, x_ref,
                  wab_ref, wpb_ref, wout_ref, mix_ref, psc_ref, g1_ref, n2_ref, sc2_ref, sh2_ref,
                  xn2_ref, ht_ref, xt_ref, *, tm):
    i = pl.program_id(0)
    la, lb, lc = l0[...], l1[...], l2[...]
    lm = jnp.maximum(jnp.maximum(la, lb), lc)
    wa, wb, wc = jnp.exp(la - lm), jnp.exp(lb - lm), jnp.exp(lc - lm)
    attn = (wa * o0[...] + wb * o1[...] + wc * o2[...]) / (wa + wb + wc)

    halo = jnp.where(i > 0, ph_ref[...], 0.0)
    p_ext = jnp.concatenate([halo, p_ref[...]], axis=0)
    t_pos = i * tm + lax.broadcasted_iota(jnp.int32, (tm, 1), 0)
    sums = p_ext
    win = 1
    pooled = []
    for g, w in enumerate(POOL_WINDOWS):
        while win < w:
            sums = sums + pltpu.roll(sums, win, axis=0)
            win *= 2
        cs = slice(g * LANES, (g + 1) * LANES)
        cnt = jnp.minimum(t_pos + 1, w).astype(F32)
        d = sums[POOL_HALO:, cs] / cnt - p_ext[POOL_HALO:, cs]
        y = jnp.dot(d.astype(BF16), mix_ref[g], preferred_element_type=F32)
        pooled.append(y)
    pool = jnp.concatenate(pooled, axis=1) * psc_ref[...]

    a_br = jnp.dot(attn.astype(BF16), wab_ref[...], preferred_element_type=F32)
    p_br = jnp.dot(pool.astype(BF16), wpb_ref[...], preferred_element_type=F32)
    merged = ga_ref[...] * a_br + gb_ref[...] * p_br
    h = x_ref[...] + g1_ref[...] * jnp.dot(merged.astype(BF16), wout_ref[...], preferred_element_type=F32)
    xn2 = _modulated_norm(h, n2_ref[...], sc2_ref[...], sh2_ref[...])
    xn2_ref[...] = xn2
    n_tiles = h.shape[1] // LANES
    for c in range(n_tiles):
        cs = slice(c * LANES, (c + 1) * LANES)
        ht_ref[pl.ds(c, tm, stride=n_tiles), :] = h[:, cs]
        xt_ref[pl.ds(c, tm, stride=n_tiles), :] = xn2[:, cs]


def _merge(os_, ls_, pool_in, gates, x, wab, wpb, wout, mix, pscale, gate1, n2g, scale2, shift2):
    s, d = x.shape
    tm = min(256, s)
    gw = GROUP_WIDTH
    blk = pl.BlockSpec((tm, gw), lambda i: (i, 0))
    vec = pl.BlockSpec((1, d), lambda i: (0, 0))

    def full(shape):
        return pl.BlockSpec(shape, lambda i: (0,) * len(shape), pipeline_mode=pl.Buffered(1))

    return pl.pallas_call(
        functools.partial(_merge_kernel, tm=tm),
        grid=(s // tm,),
        in_specs=[blk] * 6 + [
            blk,
            pl.BlockSpec((POOL_HALO, gw), lambda i: (jnp.maximum(i * (tm // POOL_HALO) - 1, 0), 0)),
            pl.BlockSpec((tm, d), lambda i: (i, 0)),
            pl.BlockSpec((tm, d), lambda i: (i, 1)),
            pl.BlockSpec((tm, d), lambda i: (i, 0)),
            full((gw, d)), full((gw, d)), full((d, d)), full(mix.shape), full((1, gw)),
            vec, vec, vec, vec],
        out_specs=[pl.BlockSpec((tm, d), lambda i: (i, 0)),
                   pl.BlockSpec((tm * (d // LANES), LANES), lambda i: (i, 0)),
                   pl.BlockSpec((tm * (d // LANES), LANES), lambda i: (i, 0))],
        out_shape=[jax.ShapeDtypeStruct((s, d), F32),
                   jax.ShapeDtypeStruct((s * (d // LANES), LANES), F32),
                   jax.ShapeDtypeStruct((s * (d // LANES), LANES), F32)],
        compiler_params=_cparams(("arbitrary",)),
        name="merge_outproj_norm2",
    )(*os_, *ls_, pool_in, pool_in, gates, gates, x, wab, wpb, wout, mix, pscale, gate1, n2g, scale2, shift2)


def _top16(s, n_rows):
    iota = lax.broadcasted_iota(jnp.int32, s.shape, 0)
    vals, poss = [], []
    for _ in range(PEER_TOPK):
        m = jnp.max(s, axis=0, keepdims=True)
        pos = jnp.min(jnp.where(s == m, iota, n_rows), axis=0, keepdims=True)
        vals.append(m)
        poss.append(pos)
        s = jnp.where(iota == pos, NEG, s)
    return vals, poss


def _route_kernel(x_ref, wq_ref, keys_ref, ids_ref, g_ref, xb_ref):
    hd = pl.program_id(1)

    @pl.when(hd == 0)
    def _():
        xb_ref[...] = x_ref[...].astype(BF16)

    qp = jnp.dot(xb_ref[...], wq_ref[...], preferred_element_type=F32)
    tops = []
    for half in range(2):
        qh = qp[:, half * LANES:(half + 1) * LANES].astype(BF16)
        st = lax.dot_general(keys_ref[0, half], qh, (((1,), (1,)), ((), ())),
                             preferred_element_type=F32)
        tops.append(_top16(st, PEER_KEYS))
    (v1, i1), (v2, i2) = tops
    v2c = jnp.concatenate(v2, axis=0)
    i2c = jnp.concatenate(i2, axis=0)
    cand = jnp.concatenate([v1[a] + v2c for a in range(PEER_TOPK)], axis=0)
    eid = jnp.concatenate([i1[a] * PEER_KEYS + i2c for a in range(PEER_TOPK)], axis=0)
    n_c = PEER_TOPK * PEER_TOPK
    iota = lax.broadcasted_iota(jnp.int32, cand.shape, 0)
    sel_v, sel_e = [], []
    for _ in range(PEER_TOPK):
        m = jnp.max(cand, axis=0, keepdims=True)
        pos = jnp.min(jnp.where(cand == m, iota, n_c), axis=0, keepdims=True)
        hit = iota == pos
        sel_v.append(m)
        sel_e.append(jnp.sum(jnp.where(hit, eid, 0), axis=0, keepdims=True))
        cand = jnp.where(hit, NEG, cand)
    sv = jnp.concatenate(sel_v, axis=0)
    ex = jnp.exp(sv - sv[0:1, :])
    g_ref[...] = ex / jnp.sum(ex, axis=0, keepdims=True)
    ids_ref[...] = jnp.concatenate(sel_e, axis=0)


def _route(xn2, wq_bf16, keys_bf16):
    s, d = xn2.shape
    tm = min(512, s)
    qd = 2 * LANES
    out_spec = pl.BlockSpec((PEER_TOPK, tm), lambda i, h: (h, i))
    return pl.pallas_call(
        _route_kernel,
        grid=(s // tm, PEER_HEADS),
        in_specs=[pl.BlockSpec((tm, d), lambda i, h: (i, 0)),
                  pl.BlockSpec((d, qd), lambda i, h: (0, h)),
                  pl.BlockSpec((1, 2, PEER_KEYS, LANES), lambda i, h: (h, 0, 0, 0))],
        out_specs=[out_spec, out_spec],
        out_shape=[jax.ShapeDtypeStruct((PEER_SLOTS, s), jnp.int32),
                   jax.ShapeDtypeStruct((PEER_SLOTS, s), F32)],
        scratch_shapes=[pltpu.VMEM((tm, d), BF16)],
        compiler_params=_cparams(("arbitrary", "arbitrary")),
        name="peer_route",
    )(xn2, wq_bf16, keys_bf16)


PEER_TOK_BLOCK = 128
PEER_SUB = 8
PEER_ROWS = PEER_SUB * PEER_SLOTS
N_SUB = PEER_TOK_BLOCK // PEER_SUB
IDS_PER_STEP = PEER_TOK_BLOCK * PEER_SLOTS
ROW_TILES = 2048 // LANES
N_ACC = 4
N_GRP = PEER_SLOTS // SUBLANES


def _pack_expert_tables(down, up):
    hi = lax.bitcast_convert_type(down.astype(jnp.bfloat16), jnp.uint16).astype(jnp.uint32)
    lo = lax.bitcast_convert_type(up.astype(jnp.bfloat16), jnp.uint16).astype(jnp.uint32)
    return ((hi << 16) | lo).reshape(-1, ROW_TILES, LANES)


def _erf_gelu(x):
    return 0.5 * x * (1.0 + lax.erf(x * (2.0 ** -0.5)))


def _peer_kernel(ids_hbm, tab_hbm, x_ref, h_ref, gates_ref, g2_ref, fg_ref, o_ref,
                 ids_smem, buf0, buf1, pscr, qscr, araw, acts, ascr, ybuf, sem_ids, sem, *, d_model, n_steps):
    step = pl.program_id(0)
    cur_ids = (step % 2) * IDS_PER_STEP
    nxt_ids = ((step + 1) % 2) * IDS_PER_STEP
    has_next_step = step + 1 < n_steps
    hi_mask = jnp.uint32(0xFFFF0000)

    def ids_copy(s, base):
        return pltpu.make_async_copy(ids_hbm.at[pl.ds(s * IDS_PER_STEP, IDS_PER_STEP)],
                                     ids_smem.at[pl.ds(base, IDS_PER_STEP)], sem_ids)

    bufs = (buf0, buf1)

    def issue_token(id_base, slot, tt, lo, hi):
        for e in range(lo, hi):
            pltpu.make_async_copy(tab_hbm.at[ids_smem[id_base + e]], bufs[slot].at[tt * PEER_SLOTS + e],
                                  sem.at[slot, tt]).start(priority=e % 2)

    def wait_token(slot, tt):
        pltpu.make_async_copy(tab_hbm.at[pl.ds(0, PEER_SLOTS)],
                              bufs[slot].at[pl.ds(tt * PEER_SLOTS, PEER_SLOTS)], sem.at[slot, tt]).wait()

    @pl.when(step == 0)
    def _():
        first = ids_copy(0, 0)
        first.start()
        first.wait()

        def body(tt, carry):
            issue_token(tt * PEER_SLOTS, 0, tt, 0, PEER_SLOTS)
            return carry
        lax.fori_loop(0, PEER_SUB, body, 0)

    @pl.when(has_next_step)
    def _():
        ids_copy(step + 1, nxt_ids).start()

    araw[...] = jnp.zeros_like(araw)
    lane = lax.broadcasted_iota(jnp.int32, (SUBLANES, LANES), 1)

    def tok_rows(t):
        return pl.ds(pl.multiple_of(t * ROW_TILES, ROW_TILES), ROW_TILES)

    def sub_block(j, slot):
        last = j == N_SUB - 1

        @pl.when(jnp.logical_and(last, has_next_step))
        def _():
            ids_copy(step + 1, nxt_ids).wait()

        following = jnp.where(last, jnp.where(has_next_step, nxt_ids, cur_ids + j * PEER_ROWS),
                              cur_ids + (j + 1) * PEER_ROWS)

        def down_token(tt, carry):
            wait_token(slot, tt)
            issue_token(following + tt * PEER_SLOTS, 1 - slot, tt, 0, PEER_SLOTS // 2)
            x = x_ref[tok_rows(j * PEER_SUB + tt), :]
            x0, x1 = x[0:SUBLANES], x[SUBLANES:]
            for g in range(N_GRP):
                for u in range(SUBLANES):
                    e = g * SUBLANES + u
                    w = bufs[slot][tt * PEER_SLOTS + e]
                    dn = lax.bitcast_convert_type(w & hi_mask, F32)
                    pscr[e * SUBLANES:(e + 1) * SUBLANES, :] = dn[0:SUBLANES] * x0 + dn[SUBLANES:] * x1
                base = g * SUBLANES * SUBLANES
                q = pscr[pl.ds(base, SUBLANES, stride=SUBLANES), :]
                for sl in range(1, SUBLANES):
                    q = q + pscr[pl.ds(base + sl, SUBLANES, stride=SUBLANES), :]
                qscr[pl.ds(pl.multiple_of(tt * PEER_SLOTS + g * SUBLANES, SUBLANES), SUBLANES), :] = q
            return carry

        for tt_static in range(PEER_SUB):
            down_token(tt_static, 0)

        for g in range(N_GRP):
            rows = slice(g * SUBLANES, (g + 1) * SUBLANES)
            v = araw[rows, :]
            for tt in range(PEER_SUB):
                a = jnp.sum(qscr[tt * PEER_SLOTS + g * SUBLANES:tt * PEER_SLOTS + (g + 1) * SUBLANES, :],
                            axis=1, keepdims=True)
                v = jnp.where(lane == j * PEER_SUB + tt, a, v)
            araw[rows, :] = v
        acts[...] = _erf_gelu(araw[...]) * gates_ref[...]
        for g in range(N_GRP):
            rows = slice(g * SUBLANES, (g + 1) * SUBLANES)
            v = acts[rows, :]
            for tt in range(PEER_SUB):
                colv = jnp.sum(jnp.where(lane == j * PEER_SUB + tt, v, 0.0), axis=1, keepdims=True)
                ascr[tt * PEER_SLOTS + g * SUBLANES:tt * PEER_SLOTS + (g + 1) * SUBLANES, :] = (
                    jnp.broadcast_to(colv, (SUBLANES, LANES)))

        def up_token(tt, carry):
            tl = j * PEER_SUB + tt
            issue_token(following + tt * PEER_SLOTS, 1 - slot, tt, PEER_SLOTS // 2, PEER_SLOTS)
            acc = [jnp.zeros((ROW_TILES, LANES), F32)] * N_ACC
            for e in range(PEER_SLOTS):
                wgt = jnp.broadcast_to(ascr[pl.ds(tt * PEER_SLOTS + e, 1), :], (ROW_TILES, LANES))
                w = bufs[slot][tt * PEER_SLOTS + e]
                up = lax.bitcast_convert_type(w << 16, F32)
                acc[e % N_ACC] = acc[e % N_ACC] + wgt * up
            ybuf[tok_rows(tl), :] = h_ref[tok_rows(tl), :] + g2_ref[...] * ((acc[0] + acc[1]) + (acc[2] + acc[3]))
            return carry

        for tt_static in range(PEER_SUB):
            up_token(tt_static, 0)

    def slot_pair(jj, carry):
        sub_block(2 * jj, 0)
        sub_block(2 * jj + 1, 1)
        return carry

    lax.fori_loop(0, N_SUB // 2, slot_pair, 0)

    chunks = [ybuf[pl.ds(c, PEER_TOK_BLOCK, stride=ROW_TILES), :] for c in range(ROW_TILES)]
    ssq = chunks[0] * chunks[0]
    for yc in chunks[1:]:
        ssq = ssq + yc * yc
    inv = lax.rsqrt(jnp.sum(ssq, axis=1, keepdims=True) / d_model + EPS)
    for c, yc in enumerate(chunks):
        cs = slice(c * LANES, (c + 1) * LANES)
        o_ref[:, cs] = yc * inv * fg_ref[:, cs]

    @pl.when(step == n_steps - 1)
    def _():
        for tt in range(PEER_SUB):
            wait_token(0, tt)


def _peer(ids_flat, table, xn2_t, h_t, gates_et, gate2_t, fg_row):
    d_model = ROW_TILES * LANES
    s = xn2_t.shape[0] // ROW_TILES
    tb = PEER_TOK_BLOCK
    n_steps = s // tb
    tok = pl.BlockSpec((tb * ROW_TILES, LANES), lambda i: (i, 0))
    return pl.pallas_call(
        functools.partial(_peer_kernel, d_model=d_model, n_steps=n_steps),
        grid=(n_steps,),
        in_specs=[pl.BlockSpec(memory_space=pl.ANY), pl.BlockSpec(memory_space=pl.ANY),
                  tok, tok, pl.BlockSpec((PEER_SLOTS, tb), lambda i: (0, i)),
                  pl.BlockSpec((ROW_TILES, LANES), lambda i: (0, 0)),
                  pl.BlockSpec((1, d_model), lambda i: (0, 0))],
        out_specs=pl.BlockSpec((tb, d_model), lambda i: (i, 0)),
        out_shape=jax.ShapeDtypeStruct((s, d_model), F32),
        scratch_shapes=[pltpu.SMEM((2 * IDS_PER_STEP,), jnp.int32),
                        pltpu.VMEM((PEER_ROWS, ROW_TILES, LANES), jnp.uint32),
                        pltpu.VMEM((PEER_ROWS, ROW_TILES, LANES), jnp.uint32),
                        pltpu.VMEM((PEER_SLOTS * SUBLANES, LANES), F32),
                        pltpu.VMEM((PEER_ROWS, LANES), F32),
                        pltpu.VMEM((PEER_SLOTS, LANES), F32),
                        pltpu.VMEM((PEER_SLOTS, LANES), F32),
                        pltpu.VMEM((PEER_ROWS, LANES), F32),
                        pltpu.VMEM((PEER_TOK_BLOCK * ROW_TILES, LANES), F32),
                        pltpu.SemaphoreType.DMA(()),
                        pltpu.SemaphoreType.DMA((2, PEER_SUB))],
        compiler_params=_cparams(("arbitrary",)),
        name="peer_experts_final_norm",
    )(ids_flat, table, xn2_t, h_t, gates_et, gate2_t, fg_row)


def kernel(x, c, ada_w, ada_b, norm1_g, w_in, pool_mix, pool_scale, w_attn_branch, w_pool_branch,
           w_out, norm2_g, peer_wq, peer_keys, peer_down, peer_up, final_g):
    b, s, d = x.shape
    assert b == 1 and ada_w.shape[0] == 1
    x2 = x[0]

    mod = _ada(c.reshape(d, 1), ada_w[0], ada_b[0].reshape(1, -1))
    shift1, scale1, gate1, shift2, scale2, gate2 = [mod[:, i * d:(i + 1) * d] for i in range(6)]

    half = HEAD_DIM // 2
    pos = jnp.arange(s, dtype=F32)
    inv = jnp.power(jnp.float32(ROPE_THETA), -jnp.arange(half, dtype=F32) / half)
    ang = pos[:, None] * inv[None, :]
    cos = jnp.concatenate([jnp.cos(ang), jnp.cos(ang)], axis=1)
    sin = jnp.concatenate([-jnp.sin(ang), jnp.sin(ang)], axis=1)

    qkv, pool_in, gates = _proj(x2, norm1_g, scale1, shift1, w_in[0].astype(BF16), cos, sin)

    outs, lses = [], []
    for gi, dil in enumerate(GROUP_DILATIONS):
        o, l = _attn_group(qkv, gi, dil)
        outs.append(o)
        lses.append(l)

    xn2, h_t, xn2_t = _merge(outs, lses, pool_in, gates, x2,
                    w_attn_branch[0].astype(BF16), w_pool_branch[0].astype(BF16), w_out[0].astype(BF16),
                    pool_mix[0].astype(BF16), pool_scale, gate1, norm2_g, scale2, shift2)

    ids_et, gates_et = _route(xn2, peer_wq[0].astype(BF16), peer_keys[0].astype(BF16))

    table = _pack_expert_tables(peer_down[0], peer_up[0])
    ids_flat = ids_et.T.reshape(-1)
    out = _peer(ids_flat, table, xn2_t, h_t, gates_et, gate2.reshape(ROW_TILES, LANES), final_g.reshape(1, d))
    return out.reshape(b, s, d)
```

```python
import functools

import jax
import jax.numpy as jnp
from jax import lax
from jax.experimental import pallas as pl
from jax.experimental.pallas import tpu as pltpu

F32 = jnp.float32
BF16 = jnp.bfloat16

HEAD_DIM = 128
N_HEADS = 12
ATTN_WIDTH = N_HEADS * HEAD_DIM
GROUP_DILATIONS = (1, 4, 16)
GROUP_WIDTH = 4 * HEAD_DIM
BAND = 128
ROPE_THETA = 10000.0
POOL_WINDOWS = (2, 4, 8, 16)
POOL_HALO = 16
PEER_HEADS = 8
PEER_KEYS = 128
PEER_TOPK = 16
PEER_SLOTS = PEER_HEADS * PEER_TOPK
EPS = 1e-6
NEG = -1e30

LANES = 128
SUBLANES = 8
VMEM_LIMIT = 56 * 1024 * 1024


def _cparams(sem):
    return pltpu.CompilerParams(dimension_semantics=sem, vmem_limit_bytes=VMEM_LIMIT)


def _ada_kernel(c_ref, w_ref, b_ref, o_ref):
    c = c_ref[...]
    s = c * jax.nn.sigmoid(c)
    o_ref[...] = jnp.sum(s * w_ref[...], axis=0, keepdims=True) + b_ref[...]


def _ada(c_col, w, b):
    d, n = w.shape
    tn = 1024
    return pl.pallas_call(
        _ada_kernel,
        grid=(n // tn,),
        in_specs=[pl.BlockSpec((d, 1), lambda j: (0, 0)),
                  pl.BlockSpec((d, tn), lambda j: (0, j)),
                  pl.BlockSpec((1, tn), lambda j: (0, j))],
        out_specs=pl.BlockSpec((1, tn), lambda j: (0, j)),
        out_shape=jax.ShapeDtypeStruct((1, n), F32),
        compiler_params=_cparams(("arbitrary",)),
        name="ada_mod",
    )(c_col, w, b)


def _modulated_norm(x, g, scale, shift):
    y = x * lax.rsqrt(jnp.mean(x * x, axis=-1, keepdims=True) + EPS)
    return y * g * (1.0 + scale) + shift


N_GROUPS = len(GROUP_DILATIONS)


def _proj_kernel(x_ref, g_ref, sc_ref, sh_ref, w_ref, cos_ref, sin_ref,
                 qkv0_ref, qkv1_ref, qkv2_ref, pool_ref, gates_ref, xn_ref, stage_ref, *, tm):
    j = pl.program_id(1)
    qkv_refs = (qkv0_ref, qkv1_ref, qkv2_ref)
    n_qkv = 3 * N_GROUPS

    @pl.when(j == 0)
    def _():
        xn_ref[...] = _modulated_norm(x_ref[...], g_ref[...], sc_ref[...], sh_ref[...]).astype(BF16)

    acc = jnp.dot(xn_ref[...], w_ref[...], preferred_element_type=F32)

    def rotary(a):
        cos = cos_ref[...]
        sin = sin_ref[...]
        heads = []
        for hh in range(GROUP_WIDTH // HEAD_DIM):
            t = a[:, hh * HEAD_DIM:(hh + 1) * HEAD_DIM]
            heads.append(t * cos + pltpu.roll(t, HEAD_DIM // 2, axis=1) * sin)
        return jnp.concatenate(heads, axis=1)

    for jj in range(n_qkv):
        @pl.when(j == jj)
        def _(jj=jj):
            tensor, grp = jj // N_GROUPS, jj % N_GROUPS
            dil = GROUP_DILATIONS[grp]
            vals = rotary(acc) if tensor < 2 else acc
            if dil == 1:
                qkv_refs[grp][0] = vals.astype(BF16)
            else:
                for cb in range(GROUP_WIDTH // LANES):
                    cs = slice(cb * LANES, (cb + 1) * LANES)
                    stage_ref[cb] = vals[:, cs]
                    for r in range(dil):
                        qkv_refs[grp][r, :, cs] = stage_ref[cb, pl.ds(r, tm // dil, stride=dil), :].astype(BF16)

    @pl.when(j == n_qkv)
    def _():
        pool_ref[...] = acc

    @pl.when(j > n_qkv)
    def _():
        gates_ref[...] = jax.nn.sigmoid(acc)


def _proj(x, g, scale, shift, w_bf16, cos, sin):
    s, d = x.shape
    n = w_bf16.shape[1]
    tn = GROUP_WIDTH
    tm = min(1024, s)
    n_qkv = 3 * N_GROUPS
    n_gate = 2 * d // tn
    nj = n // tn
    assert nj == n_qkv + 1 + n_gate and ATTN_WIDTH == N_GROUPS * tn
    vec = pl.BlockSpec((1, d), lambda i, j: (0, 0))

    def qkv_spec(grp, dil):
        return pl.BlockSpec((None, dil, tm // dil, tn),
                            lambda i, j: (jnp.clip((j - grp + N_GROUPS) // N_GROUPS - 1, 0, 2), 0, i, 0))

    return pl.pallas_call(
        functools.partial(_proj_kernel, tm=tm),
        grid=(s // tm, nj),
        in_specs=[pl.BlockSpec((tm, d), lambda i, j: (i, 0)), vec, vec, vec,
                  pl.BlockSpec((d, tn), lambda i, j: (0, j)),
                  pl.BlockSpec((tm, HEAD_DIM), lambda i, j: (i, 0)),
                  pl.BlockSpec((tm, HEAD_DIM), lambda i, j: (i, 0))],
        out_specs=[qkv_spec(grp, dil) for grp, dil in enumerate(GROUP_DILATIONS)] + [
                   pl.BlockSpec((tm, tn), lambda i, j: (i, 0)),
                   pl.BlockSpec((tm, tn), lambda i, j: (i, jnp.clip(j - n_qkv - 1, 0, n_gate - 1)))],
        out_shape=[jax.ShapeDtypeStruct((3, dil, s // dil, tn), BF16) for dil in GROUP_DILATIONS] + [
                   jax.ShapeDtypeStruct((s, tn), F32),
                   jax.ShapeDtypeStruct((s, 2 * d), F32)],
        scratch_shapes=[pltpu.VMEM((tm, d), BF16), pltpu.VMEM((tn // LANES, tm, LANES), F32)],
        compiler_params=_cparams(("arbitrary", "arbitrary")),
        name="norm1_proj",
    )(x, g, scale, shift, w_bf16, cos, sin)


def _attn_kernel(q_ref, kc_ref, kp_ref, vc_ref, vp_ref, o_ref, l_ref, *, tq):
    n = pl.program_id(1)
    row = lax.broadcasted_iota(jnp.int32, (BAND, 2 * BAND), 0)
    col = lax.broadcasted_iota(jnp.int32, (BAND, 2 * BAND), 1)
    band = jnp.logical_and(col >= row, col <= row + BAND)
    first = jnp.logical_and(band, jnp.logical_or(col >= BAND, n > 0))
    scale = HEAD_DIM ** -0.5
    for i in range(tq // BAND):
        valid = first if i == 0 else band
        for hh in range(GROUP_WIDTH // HEAD_DIM):
            cs = slice(hh * HEAD_DIM, (hh + 1) * HEAD_DIM)
            q = q_ref[i * BAND:(i + 1) * BAND, cs]
            if i == 0:
                k = jnp.concatenate([kp_ref[:, cs], kc_ref[0:BAND, cs]], axis=0)
                v = jnp.concatenate([vp_ref[:, cs], vc_ref[0:BAND, cs]], axis=0)
            else:
                k = kc_ref[(i - 1) * BAND:(i + 1) * BAND, cs]
                v = vc_ref[(i - 1) * BAND:(i + 1) * BAND, cs]
            s = lax.dot_general(q, k, (((1,), (1,)), ((), ())), preferred_element_type=F32) * scale
            s = jnp.where(valid, s, NEG)
            m = jnp.max(s, axis=-1, keepdims=True)
            p = jnp.exp(s - m)
            den = jnp.sum(p, axis=-1, keepdims=True)
            o = jnp.dot(p.astype(BF16), v, preferred_element_type=F32)
            o_ref[i * BAND:(i + 1) * BAND, cs] = o / den
            l_ref[i * BAND:(i + 1) * BAND, cs] = jnp.broadcast_to(m + jnp.log(den), (BAND, HEAD_DIM))


def _attn_group(qkv_g):
    _, dil, length, _ = qkv_g.shape
    tq = min(512, length)
    sub = tq // BAND

    def cur(t):
        return pl.BlockSpec((None, None, tq, GROUP_WIDTH), lambda r, n: (t, r, n, 0))

    def prev(t):
        return pl.BlockSpec((None, None, BAND, GROUP_WIDTH),
                            lambda r, n: (t, r, jnp.maximum(n * sub - 1, 0), 0))

    out_spec = pl.BlockSpec((None, tq, GROUP_WIDTH), lambda r, n: (r, n, 0))
    return pl.pallas_call(
        functools.partial(_attn_kernel, tq=tq),
        grid=(dil, length // tq),
        in_specs=[cur(0), cur(1), prev(1), cur(2), prev(2)],
        out_specs=[out_spec, out_spec],
        out_shape=[jax.ShapeDtypeStruct((dil, length, GROUP_WIDTH), F32)] * 2,
        compiler_params=_cparams(("arbitrary", "arbitrary")),
        name=f"dilated_attn_d{dil}",
    )(qkv_g, qkv_g, qkv_g, qkv_g, qkv_g)


def _merge_kernel(o0, o1, o2, l0, l1, l2, p_ref, ph_ref, ga_ref, gb_ref, x_ref,
                  wab_ref, wpb_ref, wout_ref, mix_ref, psc_ref, g1_ref, n2_ref, sc2_ref, sh2_ref,
                  xn2_ref, ht_ref, xt_ref, s_o1, s_o2, s_l1, s_l2, *, tm):
    i = pl.program_id(0)

    def by_position(ref, stage):
        dil = ref.shape[0]
        if dil == 1:
            return ref[0]
        cols = []
        for cb in range(GROUP_WIDTH // LANES):
            for r in range(dil):
                stage[cb, pl.ds(r, tm // dil, stride=dil), :] = ref[r, :, cb * LANES:(cb + 1) * LANES]
            cols.append(stage[cb])
        return jnp.concatenate(cols, axis=1)

    la, lb, lc = by_position(l0, None), by_position(l1, s_l1), by_position(l2, s_l2)
    lm = jnp.maximum(jnp.maximum(la, lb), lc)
    wa, wb, wc = jnp.exp(la - lm), jnp.exp(lb - lm), jnp.exp(lc - lm)
    attn = (wa * by_position(o0, None) + wb * by_position(o1, s_o1) + wc * by_position(o2, s_o2)) / (wa + wb + wc)

    halo = jnp.where(i > 0, ph_ref[...], 0.0)
    p_ext = jnp.concatenate([halo, p_ref[...]], axis=0)
    t_pos = i * tm + lax.broadcasted_iota(jnp.int32, (tm, 1), 0)
    sums = p_ext
    win = 1
    pooled = []
    for g, w in enumerate(POOL_WINDOWS):
        while win < w:
            sums = sums + pltpu.roll(sums, win, axis=0)
            win *= 2
        cs = slice(g * LANES, (g + 1) * LANES)
        cnt = jnp.minimum(t_pos + 1, w).astype(F32)
        d = sums[POOL_HALO:, cs] / cnt - p_ext[POOL_HALO:, cs]
        y = jnp.dot(d.astype(BF16), mix_ref[g], preferred_element_type=F32)
        pooled.append(y)
    pool = jnp.concatenate(pooled, axis=1) * psc_ref[...]

    a_br = jnp.dot(attn.astype(BF16), wab_ref[...], preferred_element_type=F32)
    p_br = jnp.dot(pool.astype(BF16), wpb_ref[...], preferred_element_type=F32)
    merged = ga_ref[...] * a_br + gb_ref[...] * p_br
    h = x_ref[...] + g1_ref[...] * jnp.dot(merged.astype(BF16), wout_ref[...], preferred_element_type=F32)
    xn2 = _modulated_norm(h, n2_ref[...], sc2_ref[...], sh2_ref[...])
    xn2_ref[...] = xn2
    n_tiles = h.shape[1] // LANES
    for c in range(n_tiles):
        cs = slice(c * LANES, (c + 1) * LANES)
        ht_ref[pl.ds(c, tm, stride=n_tiles), :] = h[:, cs]
        xt_ref[pl.ds(c, tm, stride=n_tiles), :] = xn2[:, cs]


def _merge(os_, ls_, pool_in, gates, x, wab, wpb, wout, mix, pscale, gate1, n2g, scale2, shift2):
    s, d = x.shape
    tm = min(256, s)
    gw = GROUP_WIDTH
    blk = pl.BlockSpec((tm, gw), lambda i: (i, 0))
    vec = pl.BlockSpec((1, d), lambda i: (0, 0))

    def full(shape):
        return pl.BlockSpec(shape, lambda i: (0,) * len(shape), pipeline_mode=pl.Buffered(1))

    def grp(dil):
        return pl.BlockSpec((dil, tm // dil, gw), lambda i: (0, i, 0))

    return pl.pallas_call(
        functools.partial(_merge_kernel, tm=tm),
        grid=(s // tm,),
        scratch_shapes=[pltpu.VMEM((gw // LANES, tm, LANES), F32)] * 4,
        in_specs=[grp(dil) for dil in GROUP_DILATIONS] * 2 + [
            blk,
            pl.BlockSpec((POOL_HALO, gw), lambda i: (jnp.maximum(i * (tm // POOL_HALO) - 1, 0), 0)),
            pl.BlockSpec((tm, d), lambda i: (i, 0)),
            pl.BlockSpec((tm, d), lambda i: (i, 1)),
            pl.BlockSpec((tm, d), lambda i: (i, 0)),
            full((gw, d)), full((gw, d)), full((d, d)), full(mix.shape), full((1, gw)),
            vec, vec, vec, vec],
        out_specs=[pl.BlockSpec((tm, d), lambda i: (i, 0)),
                   pl.BlockSpec((tm * (d // LANES), LANES), lambda i: (i, 0)),
                   pl.BlockSpec((tm * (d // LANES), LANES), lambda i: (i, 0))],
        out_shape=[jax.ShapeDtypeStruct((s, d), F32),
                   jax.ShapeDtypeStruct((s * (d // LANES), LANES), F32),
                   jax.ShapeDtypeStruct((s * (d // LANES), LANES), F32)],
        compiler_params=_cparams(("arbitrary",)),
        name="merge_outproj_norm2",
    )(*os_, *ls_, pool_in, pool_in, gates, gates, x, wab, wpb, wout, mix, pscale, gate1, n2g, scale2, shift2)


def _top16(s, n_rows):
    iota = lax.broadcasted_iota(jnp.int32, s.shape, 0)
    vals, poss = [], []
    for _ in range(PEER_TOPK):
        m = jnp.max(s, axis=0, keepdims=True)
        pos = jnp.min(jnp.where(s == m, iota, n_rows), axis=0, keepdims=True)
        vals.append(m)
        poss.append(pos)
        s = jnp.where(iota == pos, NEG, s)
    return vals, poss


def _route_kernel(x_ref, wq_ref, keys_ref, ids_ref, g_ref, xb_ref):
    hd = pl.program_id(1)

    @pl.when(hd == 0)
    def _():
        xb_ref[...] = x_ref[...].astype(BF16)

    qp = jnp.dot(xb_ref[...], wq_ref[...], preferred_element_type=F32)
    tops = []
    for half in range(2):
        qh = qp[:, half * LANES:(half + 1) * LANES].astype(BF16)
        st = lax.dot_general(keys_ref[0, half], qh, (((1,), (1,)), ((), ())),
                             preferred_element_type=F32)
        tops.append(_top16(st, PEER_KEYS))
    (v1, i1), (v2, i2) = tops
    v2c = jnp.concatenate(v2, axis=0)
    i2c = jnp.concatenate(i2, axis=0)
    half_k = PEER_TOPK // 2
    b_iota = lax.broadcasted_iota(jnp.int32, (half_k, v2c.shape[1]), 0)
    cands = [v1[0] + v2c]
    eids = [i1[0] * PEER_KEYS + i2c]
    for a in range(1, half_k):
        ok = b_iota < PEER_TOPK // (a + 1)
        cands.append(jnp.where(ok, v1[a] + v2c[0:half_k], NEG))
        eids.append(i1[a] * PEER_KEYS + i2c[0:half_k])
    cands.append(jnp.concatenate(v1[half_k:], axis=0) + v2[0])
    eids.append(jnp.concatenate(i1[half_k:], axis=0) * PEER_KEYS + i2[0])
    cand = jnp.concatenate(cands, axis=0)
    eid = jnp.concatenate(eids, axis=0)
    n_c = cand.shape[0]
    iota = lax.broadcasted_iota(jnp.int32, cand.shape, 0)
    sel_v, sel_e = [], []
    for _ in range(PEER_TOPK):
        m = jnp.max(cand, axis=0, keepdims=True)
        pos = jnp.min(jnp.where(cand == m, iota, n_c), axis=0, keepdims=True)
        hit = iota == pos
        sel_v.append(m)
        sel_e.append(jnp.sum(jnp.where(hit, eid, 0), axis=0, keepdims=True))
        cand = jnp.where(hit, NEG, cand)
    sv = jnp.concatenate(sel_v, axis=0)
    ex = jnp.exp(sv - sv[0:1, :])
    g_ref[...] = ex / jnp.sum(ex, axis=0, keepdims=True)
    ids_ref[...] = jnp.concatenate(sel_e, axis=0)


def _route(xn2, wq_bf16, keys_bf16):
    s, d = xn2.shape
    tm = min(512, s)
    qd = 2 * LANES
    out_spec = pl.BlockSpec((PEER_TOPK, tm), lambda i, h: (h, i))
    return pl.pallas_call(
        _route_kernel,
        grid=(s // tm, PEER_HEADS),
        in_specs=[pl.BlockSpec((tm, d), lambda i, h: (i, 0)),
                  pl.BlockSpec((d, qd), lambda i, h: (0, h)),
                  pl.BlockSpec((1, 2, PEER_KEYS, LANES), lambda i, h: (h, 0, 0, 0))],
        out_specs=[out_spec, out_spec],
        out_shape=[jax.ShapeDtypeStruct((PEER_SLOTS, s), jnp.int32),
                   jax.ShapeDtypeStruct((PEER_SLOTS, s), F32)],
        scratch_shapes=[pltpu.VMEM((tm, d), BF16)],
        compiler_params=_cparams(("arbitrary", "arbitrary")),
        name="peer_route",
    )(xn2, wq_bf16, keys_bf16)


PEER_TOK_BLOCK = 128
PEER_SUB = 8
PEER_ROWS = PEER_SUB * PEER_SLOTS
N_SUB = PEER_TOK_BLOCK // PEER_SUB
ROW_TILES = 2048 // LANES
N_ACC = 4
N_GRP = PEER_SLOTS // SUBLANES


def _pack_expert_tables(down, up):
    e = down.shape[0]
    return jnp.concatenate([down.astype(jnp.bfloat16).reshape(e, ROW_TILES, LANES),
                            up.astype(jnp.bfloat16).reshape(e, ROW_TILES, LANES)], axis=1)


def _erf_gelu(x):
    return 0.5 * x * (1.0 + lax.erf(x * (2.0 ** -0.5)))


def _peer_kernel(ids_hbm, tab_hbm, x_ref, h_ref, gates_ref, g2_ref, fg_ref, o_ref,
                 ids_smem, buf0, buf1, pscr, qscr, araw, acts, ascr, ybuf, sem_ids, sem, *, d_model, n_steps):
    step = pl.program_id(0)
    last_sub = n_steps * N_SUB - 1

    def ids_copy(k, half):
        k = jnp.minimum(k, last_sub)
        return pltpu.make_async_copy(ids_hbm.at[pl.ds(k * PEER_ROWS, PEER_ROWS)],
                                     ids_smem.at[pl.ds(half * PEER_ROWS, PEER_ROWS)], sem_ids.at[half])

    bufs = (buf0, buf1)

    def issue_token(slot, tt, lo, hi):
        for e in range(lo, hi):
            pltpu.make_async_copy(tab_hbm.at[ids_smem[slot * PEER_ROWS + tt * PEER_SLOTS + e]],
                                  bufs[slot].at[tt * PEER_SLOTS + e], sem.at[slot, tt]).start(priority=e % 2)

    def wait_token(slot, tt):
        pltpu.make_async_copy(tab_hbm.at[pl.ds(0, PEER_SLOTS)],
                              bufs[slot].at[pl.ds(tt * PEER_SLOTS, PEER_SLOTS)], sem.at[slot, tt]).wait()

    @pl.when(step == 0)
    def _():
        first = ids_copy(0, 0)
        first.start()
        first.wait()
        for tt in range(PEER_SUB):
            issue_token(0, tt, 0, PEER_SLOTS)
        ids_copy(1, 1).start()

    araw[...] = jnp.zeros_like(araw)
    lane = lax.broadcasted_iota(jnp.int32, (SUBLANES, LANES), 1)

    def tok_rows(t):
        return pl.ds(pl.multiple_of(t * ROW_TILES, ROW_TILES), ROW_TILES)

    def sub_block(j, slot):
        k = step * N_SUB + j
        ids_copy(k + 1, 1 - slot).wait()
        ids_copy(k + 2, slot).start()

        def down_token(tt, carry):
            wait_token(slot, tt)
            issue_token(1 - slot, tt, 0, PEER_SLOTS // 2)
            x = x_ref[tok_rows(j * PEER_SUB + tt), :]
            x0, x1 = x[0:SUBLANES], x[SUBLANES:]
            for g in range(N_GRP):
                for u in range(SUBLANES):
                    e = g * SUBLANES + u
                    dn = bufs[slot][tt * PEER_SLOTS + e, 0:ROW_TILES, :].astype(F32)
                    pscr[e * SUBLANES:(e + 1) * SUBLANES, :] = dn[0:SUBLANES] * x0 + dn[SUBLANES:] * x1
                base = g * SUBLANES * SUBLANES
                q = pscr[pl.ds(base, SUBLANES, stride=SUBLANES), :]
                for sl in range(1, SUBLANES):
                    q = q + pscr[pl.ds(base + sl, SUBLANES, stride=SUBLANES), :]
                qscr[pl.ds(pl.multiple_of(tt * PEER_SLOTS + g * SUBLANES, SUBLANES), SUBLANES), :] = q
            return carry

        for tt_static in range(PEER_SUB):
            down_token(tt_static, 0)

        for g in range(N_GRP):
            rows = slice(g * SUBLANES, (g + 1) * SUBLANES)
            v = araw[rows, :]
            for tt in range(PEER_SUB):
                a = jnp.sum(qscr[tt * PEER_SLOTS + g * SUBLANES:tt * PEER_SLOTS + (g + 1) * SUBLANES, :],
                            axis=1, keepdims=True)
                v = jnp.where(lane == j * PEER_SUB + tt, a, v)
            araw[rows, :] = v
        acts[...] = _erf_gelu(araw[...]) * gates_ref[...]
        for g in range(N_GRP):
            rows = slice(g * SUBLANES, (g + 1) * SUBLANES)
            v = acts[rows, :]
            for tt in range(PEER_SUB):
                colv = jnp.sum(jnp.where(lane == j * PEER_SUB + tt, v, 0.0), axis=1, keepdims=True)
                ascr[tt * PEER_SLOTS + g * SUBLANES:tt * PEER_SLOTS + (g + 1) * SUBLANES, :] = (
                    jnp.broadcast_to(colv, (SUBLANES, LANES)))

        def up_token(tt, carry):
            tl = j * PEER_SUB + tt
            issue_token(1 - slot, tt, PEER_SLOTS // 2, PEER_SLOTS)
            acc = [jnp.zeros((ROW_TILES, LANES), F32)] * N_ACC
            for e in range(PEER_SLOTS):
                wgt = jnp.broadcast_to(ascr[pl.ds(tt * PEER_SLOTS + e, 1), :], (ROW_TILES, LANES))
                up = bufs[slot][tt * PEER_SLOTS + e, ROW_TILES:, :].astype(F32)
                acc[e % N_ACC] = acc[e % N_ACC] + wgt * up
            ybuf[tok_rows(tl), :] = h_ref[tok_rows(tl), :] + g2_ref[...] * ((acc[0] + acc[1]) + (acc[2] + acc[3]))
            return carry

        for tt_static in range(PEER_SUB):
            up_token(tt_static, 0)

    def slot_pair(jj, carry):
        sub_block(2 * jj, 0)
        sub_block(2 * jj + 1, 1)
        return carry

    lax.fori_loop(0, N_SUB // 2, slot_pair, 0)

    chunks = [ybuf[pl.ds(c, PEER_TOK_BLOCK, stride=ROW_TILES), :] for c in range(ROW_TILES)]
    ssq = chunks[0] * chunks[0]
    for yc in chunks[1:]:
        ssq = ssq + yc * yc
    inv = lax.rsqrt(jnp.sum(ssq, axis=1, keepdims=True) / d_model + EPS)
    for c, yc in enumerate(chunks):
        cs = slice(c * LANES, (c + 1) * LANES)
        o_ref[:, cs] = yc * inv * fg_ref[:, cs]

    @pl.when(step == n_steps - 1)
    def _():
        for tt in range(PEER_SUB):
            wait_token(0, tt)
        ids_copy(last_sub, 1).wait()


def _peer(ids_flat, table, xn2_t, h_t, gates_et, gate2_t, fg_row):
    d_model = ROW_TILES * LANES
    s = xn2_t.shape[0] // ROW_TILES
    tb = PEER_TOK_BLOCK
    n_steps = s // tb
    tok = pl.BlockSpec((tb * ROW_TILES, LANES), lambda i: (i, 0))
    return pl.pallas_call(
        functools.partial(_peer_kernel, d_model=d_model, n_steps=n_steps),
        grid=(n_steps,),
        in_specs=[pl.BlockSpec(memory_space=pl.ANY), pl.BlockSpec(memory_space=pl.ANY),
                  tok, tok, pl.BlockSpec((PEER_SLOTS, tb), lambda i: (0, i)),
                  pl.BlockSpec((ROW_TILES, LANES), lambda i: (0, 0)),
                  pl.BlockSpec((1, d_model), lambda i: (0, 0))],
        out_specs=pl.BlockSpec((tb, d_model), lambda i: (i, 0)),
        out_shape=jax.ShapeDtypeStruct((s, d_model), F32),
        scratch_shapes=[pltpu.SMEM((2 * PEER_ROWS,), jnp.int32),
                        pltpu.VMEM((PEER_ROWS, 2 * ROW_TILES, LANES), jnp.bfloat16),
                        pltpu.VMEM((PEER_ROWS, 2 * ROW_TILES, LANES), jnp.bfloat16),
                        pltpu.VMEM((PEER_SLOTS * SUBLANES, LANES), F32),
                        pltpu.VMEM((PEER_ROWS, LANES), F32),
                        pltpu.VMEM((PEER_SLOTS, LANES), F32),
                        pltpu.VMEM((PEER_SLOTS, LANES), F32),
                        pltpu.VMEM((PEER_ROWS, LANES), F32),
                        pltpu.VMEM((PEER_TOK_BLOCK * ROW_TILES, LANES), F32),
                        pltpu.SemaphoreType.DMA((2,)),
                        pltpu.SemaphoreType.DMA((2, PEER_SUB))],
        compiler_params=_cparams(("arbitrary",)),
        name="peer_experts_final_norm",
    )(ids_flat, table, xn2_t, h_t, gates_et, gate2_t, fg_row)


def kernel(x, c, ada_w, ada_b, norm1_g, w_in, pool_mix, pool_scale, w_attn_branch, w_pool_branch,
           w_out, norm2_g, peer_wq, peer_keys, peer_down, peer_up, final_g):
    b, s, d = x.shape
    assert b == 1 and ada_w.shape[0] == 1
    x2 = x[0]

    mod = _ada(c.reshape(d, 1), ada_w[0], ada_b[0].reshape(1, -1))
    shift1, scale1, gate1, shift2, scale2, gate2 = [mod[:, i * d:(i + 1) * d] for i in range(6)]

    half = HEAD_DIM // 2
    pos = jnp.arange(s, dtype=F32)
    inv = jnp.power(jnp.float32(ROPE_THETA), -jnp.arange(half, dtype=F32) / half)
    ang = pos[:, None] * inv[None, :]
    cos = jnp.concatenate([jnp.cos(ang), jnp.cos(ang)], axis=1)
    sin = jnp.concatenate([-jnp.sin(ang), jnp.sin(ang)], axis=1)

    *qkv_groups, pool_in, gates = _proj(x2, norm1_g, scale1, shift1, w_in[0].astype(BF16), cos, sin)

    outs, lses = [], []
    for qkv_g in qkv_groups:
        o, l = _attn_group(qkv_g)
        outs.append(o)
        lses.append(l)

    xn2, h_t, xn2_t = _merge(outs, lses, pool_in, gates, x2,
                    w_attn_branch[0].astype(BF16), w_pool_branch[0].astype(BF16), w_out[0].astype(BF16),
                    pool_mix[0].astype(BF16), pool_scale, gate1, norm2_g, scale2, shift2)

    ids_et, gates_et = _route(xn2, peer_wq[0].astype(BF16), peer_keys[0].astype(BF16))

    table = _pack_expert_tables(peer_down[0], peer_up[0])
    ids_flat = ids_et.T.reshape(-1)
    out = _peer(ids_flat, table, xn2_t, h_t, gates_et, gate2.reshape(ROW_TILES, LANES), final_g.reshape(1, d))
    return out.reshape(b, s, d)
```
